```python
import math
import jax
import jax.numpy as jnp
from jax import lax
import numpy as np

D_MODEL = 2048
BATCH = 4
SEQ = 2048
DEPTH = 4
DEC_BATCH = 128
DEC_SEQ = 8
PAST_LEN = 16384
PAGE_SIZE = 128

HEAD_DIM = 128
MIX_WIDTH = D_MODEL
GDN_HEADS = 6
MLSTM_HEADS = 5
RET_HEADS = 5
GDN_WIDTH = GDN_HEADS * HEAD_DIM
MLSTM_WIDTH = MLSTM_HEADS * HEAD_DIM
RET_WIDTH = RET_HEADS * HEAD_DIM
GDN_QKV = 3 * GDN_WIDTH
CONV_W = 4
CHUNK = 64
FFN_HIDDEN = ((8 * D_MODEL + 3 * 256 - 1) // (3 * 256)) * 256
IN_SIZES = (GDN_QKV, GDN_WIDTH, GDN_HEADS, GDN_HEADS,
            3 * MLSTM_WIDTH, MLSTM_WIDTH, MLSTM_HEADS, MLSTM_HEADS,
            3 * RET_WIDTH, RET_WIDTH)
IN_WIDTH = 4 * GDN_WIDTH + 2 * GDN_HEADS + 4 * MLSTM_WIDTH + 2 * MLSTM_HEADS + 4 * RET_WIDTH
EPS = 1e-6
ROPE_BASE = 10000.0
RET_DECAY_OFFSET = 5

kernel_name = 'hymba_gdn_mlstm_retention_step'


def _split_points():
    pts, acc = [], 0
    for s in IN_SIZES[:-1]:
        acc += s
        pts.append(acc)
    return pts


def _rmsnorm(x, g):
    xf = x.astype(jnp.float32)
    y = xf * lax.rsqrt(jnp.mean(xf * xf, axis=-1, keepdims=True) + EPS)
    return (y * g.astype(jnp.float32)).astype(x.dtype)


def _groupnorm(x, g):
    xf = x.astype(jnp.float32)
    mu = jnp.mean(xf, axis=-1, keepdims=True)
    var = jnp.mean(jnp.square(xf - mu), axis=-1, keepdims=True)
    return (xf - mu) * lax.rsqrt(var + EPS) * g.astype(jnp.float32)


def _l2norm(x):
    return x * lax.rsqrt(jnp.sum(x * x, axis=-1, keepdims=True) + EPS)


def _rope(x, pos):
    half = HEAD_DIM // 2
    inv = ROPE_BASE ** (-jnp.arange(half, dtype=jnp.float32) / half)
    ang = pos.astype(jnp.float32)[:, None] * inv[None, :]
    cos = jnp.cos(ang)[None, :, None, :]
    sin = jnp.sin(ang)[None, :, None, :]
    x1, x2 = x[..., :half], x[..., half:]
    return jnp.concatenate([x1 * cos - x2 * sin, x1 * sin + x2 * cos], axis=-1)


def _to_chunks(a, L):
    B, H, T = a.shape[:3]
    a = a.reshape((B, H, T // L, L) + a.shape[3:])
    return jnp.moveaxis(a, 2, 0)


def _from_chunks(a):
    N, B, H, L = a.shape[:4]
    return jnp.moveaxis(a, 0, 2).reshape((B, H, N * L) + a.shape[4:])


def _gated_delta_chunked(q, k, v, beta, log_alpha, S0, L):
    idx = jnp.arange(L)
    strict = idx[:, None] > idx[None, :]
    incl = idx[:, None] >= idx[None, :]
    qc, kc, vc = _to_chunks(q, L), _to_chunks(k, L), _to_chunks(v, L)
    bc = _to_chunks(beta, L)
    gc = jnp.cumsum(_to_chunks(log_alpha, L), axis=-1)
    decay = jnp.exp(jnp.where(incl, gc[..., :, None] - gc[..., None, :], -jnp.inf))
    kb = kc * bc[..., None]
    a_strict = jnp.where(strict, jnp.einsum('nbhid,nbhjd->nbhij', kb, kc) * decay, 0.0)
    tmat = a_strict + jnp.eye(L, dtype=q.dtype)
    u = lax.linalg.triangular_solve(tmat, vc * bc[..., None], left_side=True, lower=True, unit_diagonal=True)
    w = lax.linalg.triangular_solve(tmat, kb * jnp.exp(gc)[..., None], left_side=True, lower=True, unit_diagonal=True)
    qk = jnp.einsum('nbhid,nbhjd->nbhij', qc, kc) * decay

    def step(S, inp):
        q_i, k_i, u_i, w_i, g_i, qk_i = inp
        v_new = u_i - jnp.einsum('bhld,bhdv->bhlv', w_i, S)
        o = (jnp.einsum('bhld,bhdv->bhlv', q_i * jnp.exp(g_i)[..., None], S)
             + jnp.einsum('bhlm,bhmv->bhlv', qk_i, v_new))
        g_last = g_i[..., -1:]
        S = (jnp.exp(g_last)[..., None] * S
             + jnp.einsum('bhld,bhlv->bhdv', k_i * jnp.exp(g_last - g_i)[..., None], v_new))
        return S, o

    S, o = lax.scan(step, S0, (qc, kc, u, w, gc, qk))
    return _from_chunks(o), S


def _mlstm_chunked(q, k, v, ig, lf, C0, n0, m0, L):
    idx = jnp.arange(L)
    incl = idx[:, None] >= idx[None, :]
    qc, kc, vc = _to_chunks(q, L), _to_chunks(k, L), _to_chunks(v, L)
    igc, lfc = _to_chunks(ig, L), _to_chunks(lf, L)

    def step(carry, inp):
        C, n, m = carry
        q_i, k_i, v_i, ig_i, lf_i = inp
        b = jnp.cumsum(lf_i, axis=-1)
        logd = jnp.where(incl, b[..., :, None] - b[..., None, :] + ig_i[..., None, :], -jnp.inf)
        inter = b + m[..., None]
        m_t = jnp.maximum(inter, jnp.max(logd, axis=-1))
        s = jnp.einsum('bhld,bhmd->bhlm', q_i, k_i) * jnp.exp(logd - m_t[..., None])
        w_inter = jnp.exp(inter - m_t)
        num = (jnp.einsum('bhlm,bhmv->bhlv', s, v_i)
               + w_inter[..., None] * jnp.einsum('bhld,bhdv->bhlv', q_i, C))
        den = jnp.sum(s, axis=-1) + w_inter * jnp.einsum('bhld,bhd->bhl', q_i, n)
        h = num / jnp.maximum(jnp.abs(den), jnp.exp(-m_t))[..., None]
        m_new = m_t[..., -1]
        w_key = jnp.exp(logd[..., -1, :] - m_new[..., None])
        dec = jnp.exp(inter[..., -1] - m_new)
        kw = k_i * w_key[..., None]
        C = dec[..., None, None] * C + jnp.einsum('bhld,bhlv->bhdv', kw, v_i)
        n = dec[..., None] * n + jnp.sum(kw, axis=-2)
        return (C, n, m_new), h

    (C, n, m), h = lax.scan(step, (C0, n0, m0), (qc, kc, vc, igc, lfc))
    return _from_chunks(h), (C, n, m)


def _retention_chunked(q, k, v, log_gamma, S0, L):
    idx = jnp.arange(L, dtype=jnp.float32)
    rel = idx[:, None] - idx[None, :]
    lg = log_gamma[:, None, None]
    dmat = jnp.where(rel >= 0, jnp.exp(jnp.maximum(rel, 0.0) * lg), 0.0)
    q_decay = jnp.exp((idx + 1.0)[None, :] * log_gamma[:, None])
    k_decay = jnp.exp((L - 1.0 - idx)[None, :] * log_gamma[:, None])
    chunk_decay = jnp.exp(L * log_gamma)
    qc, kc, vc = _to_chunks(q, L), _to_chunks(k, L), _to_chunks(v, L)

    def step(S, inp):
        q_i, k_i, v_i = inp
        s = jnp.einsum('bhld,bhmd->bhlm', q_i, k_i) * dmat
        o = (jnp.einsum('bhlm,bhmv->bhlv', s, v_i)
             + jnp.einsum('bhld,bhdv->bhlv', q_i * q_decay[:, :, None], S))
        S = (chunk_decay[:, None, None] * S
             + jnp.einsum('bhld,bhlv->bhdv', k_i * k_decay[:, :, None], v_i))
        return S, o

    S, o = lax.scan(step, S0, (qc, kc, vc))
    return _from_chunks(o), S


def _mixer(h, conv_buf, gdn_S0, mC0, mn0, mm0, retS0, pos0,
           w_in, conv_w, a_log, dt_bias, gdn_g, ig_b, fg_b, mlstm_g, ret_g, w_out):
    B, T, _ = h.shape
    L = math.gcd(T, CHUNK)
    f32 = jnp.float32
    proj = jnp.einsum('btd,dp->btp', h, w_in)
    (g_qkv, g_gate, g_beta, g_a, m_qkv, m_o, m_i, m_f, r_qkv, r_gate) = jnp.split(proj, _split_points(), axis=-1)

    xpad = jnp.concatenate([conv_buf.astype(g_qkv.dtype), g_qkv], axis=1)
    conv = xpad[:, 0:T] * conv_w[0]
    for w in range(1, CONV_W):
        conv = conv + xpad[:, w:w + T] * conv_w[w]
    conv = jax.nn.silu(conv).astype(f32)
    new_buf = xpad[:, T:]
    gq, gk, gv = [a.reshape(B, T, GDN_HEADS, HEAD_DIM).transpose(0, 2, 1, 3) for a in jnp.split(conv, 3, axis=-1)]
    gq = _l2norm(gq) * (HEAD_DIM ** -0.5)
    gk = _l2norm(gk)
    beta = jax.nn.sigmoid(g_beta.astype(f32)).transpose(0, 2, 1)
    log_alpha = (-jnp.exp(a_log.astype(f32))
                 * jax.nn.softplus(g_a.astype(f32) + dt_bias.astype(f32))).transpose(0, 2, 1)
    go, gS = _gated_delta_chunked(gq, gk, gv, beta, log_alpha, gdn_S0.astype(f32), L)
    go = _rmsnorm(go.transpose(0, 2, 1, 3), gdn_g) * jax.nn.silu(g_gate.astype(f32).reshape(B, T, GDN_HEADS, HEAD_DIM))
    go = go.reshape(B, T, GDN_WIDTH)

    mq, mk, mv = [a.astype(f32).reshape(B, T, MLSTM_HEADS, HEAD_DIM).transpose(0, 2, 1, 3)
                  for a in jnp.split(m_qkv, 3, axis=-1)]
    mk = mk * (HEAD_DIM ** -0.5)
    ig = (m_i.astype(f32) + ig_b.astype(f32)).transpose(0, 2, 1)
    lf = jax.nn.log_sigmoid(m_f.astype(f32) + fg_b.astype(f32)).transpose(0, 2, 1)
    mh, (mC, mn, mm) = _mlstm_chunked(mq, mk, mv, ig, lf, mC0.astype(f32), mn0.astype(f32), mm0.astype(f32), L)
    mh = _rmsnorm(mh.transpose(0, 2, 1, 3), mlstm_g) * jax.nn.sigmoid(m_o.astype(f32).reshape(B, T, MLSTM_HEADS, HEAD_DIM))
    mh = mh.reshape(B, T, MLSTM_WIDTH)

    rq, rk, rv = [a.astype(f32).reshape(B, T, RET_HEADS, HEAD_DIM) for a in jnp.split(r_qkv, 3, axis=-1)]
    pos = pos0 + jnp.arange(T)
    rq = _rope(rq, pos).transpose(0, 2, 1, 3)
    rk = (_rope(rk, pos) * (HEAD_DIM ** -0.5)).transpose(0, 2, 1, 3)
    rv = rv.transpose(0, 2, 1, 3)
    log_gamma = jnp.log1p(-jnp.exp2(-(RET_DECAY_OFFSET + jnp.arange(RET_HEADS, dtype=f32))))
    ro, rS = _retention_chunked(rq, rk, rv, log_gamma, retS0.astype(f32), L)
    ro = _groupnorm(ro.transpose(0, 2, 1, 3), ret_g) * jax.nn.silu(r_gate.astype(f32).reshape(B, T, RET_HEADS, HEAD_DIM))
    ro = ro.reshape(B, T, RET_WIDTH)

    mix = jnp.concatenate([go, mh, ro], axis=-1).astype(h.dtype)
    out = jnp.einsum('btm,md->btd', mix, w_out)
    dt = h.dtype
    return out, (new_buf.astype(dt), gS.astype(dt), mC.astype(dt), mn.astype(dt), mm.astype(dt), rS.astype(dt))


def _layer(x, state, pos0, attn_g, w_in, conv_w, a_log, dt_bias, gdn_g, ig_b, fg_b, mlstm_g, ret_g,
           w_out, ffn_g, w_gate, w_up, w_down):
    h = _rmsnorm(x, attn_g)
    mix, new_state = _mixer(h, state[0], state[1], state[2], state[3], state[4], state[5], pos0,
                            w_in, conv_w, a_log, dt_bias, gdn_g, ig_b, fg_b, mlstm_g, ret_g, w_out)
    x = x + mix
    h = _rmsnorm(x, ffn_g)
    ff = jax.nn.silu(jnp.einsum('btd,df->btf', h, w_gate)) * jnp.einsum('btd,df->btf', h, w_up)
    x = x + jnp.einsum('btf,fd->btd', ff, w_down)
    return x, new_state


def setup_inputs(seed: int = 0) -> dict:
    key = jax.random.key(seed)
    ks = jax.random.split(key, 24)
    f32 = jnp.float32

    def nrm(k, shape, scale):
        return jax.random.normal(k, shape, f32) * scale

    dt = jnp.exp(jax.random.uniform(ks[12], (DEPTH, GDN_HEADS), f32, math.log(1e-3), math.log(1e-1)))
    return {
        'x_prompt': nrm(ks[0], (BATCH, SEQ, D_MODEL), 1.0),
        'x_sample': nrm(ks[1], (DEC_BATCH, DEC_SEQ, D_MODEL), 1.0),
        'state_gdn_conv': nrm(ks[2], (DEPTH, DEC_BATCH, CONV_W - 1, GDN_QKV), 1.0),
        'state_gdn_S': nrm(ks[3], (DEPTH, DEC_BATCH, GDN_HEADS, HEAD_DIM, HEAD_DIM), HEAD_DIM ** -0.5),
        'state_mlstm_C': nrm(ks[4], (DEPTH, DEC_BATCH, MLSTM_HEADS, HEAD_DIM, HEAD_DIM), HEAD_DIM ** -0.5),
        'state_mlstm_n': nrm(ks[5], (DEPTH, DEC_BATCH, MLSTM_HEADS, HEAD_DIM), HEAD_DIM ** -0.5),
        'state_mlstm_m': nrm(ks[6], (DEPTH, DEC_BATCH, MLSTM_HEADS), 1.0),
        'state_ret_S': nrm(ks[7], (DEPTH, DEC_BATCH, RET_HEADS, HEAD_DIM, HEAD_DIM), 1.0),
        'attn_norm_g': 1.0 + nrm(ks[8], (DEPTH, D_MODEL), 0.02),
        'w_in': nrm(ks[9], (DEPTH, D_MODEL, IN_WIDTH), D_MODEL ** -0.5),
        'gdn_conv_w': nrm(ks[10], (DEPTH, CONV_W, GDN_QKV), CONV_W ** -0.5),
        'gdn_a_log': jnp.log(jax.random.uniform(ks[11], (DEPTH, GDN_HEADS), f32, 1.0, 16.0)),
        'gdn_dt_bias': dt + jnp.log(-jnp.expm1(-dt)),
        'gdn_norm_g': 1.0 + nrm(ks[13], (DEPTH, HEAD_DIM), 0.02),
        'mlstm_ig_bias': nrm(ks[14], (DEPTH, MLSTM_HEADS), 0.1),
        'mlstm_fg_bias': 3.0 + 3.0 * jax.random.uniform(ks[15], (DEPTH, MLSTM_HEADS), f32),
        'mlstm_norm_g': 1.0 + nrm(ks[16], (DEPTH, MLSTM_HEADS, HEAD_DIM), 0.02),
        'ret_norm_g': 1.0 + nrm(ks[17], (DEPTH, RET_HEADS, HEAD_DIM), 0.02),
        'w_out': nrm(ks[18], (DEPTH, MIX_WIDTH, D_MODEL), MIX_WIDTH ** -0.5),
        'ffn_norm_g': 1.0 + nrm(ks[19], (DEPTH, D_MODEL), 0.02),
        'w_gate': nrm(ks[20], (DEPTH, D_MODEL, FFN_HIDDEN), D_MODEL ** -0.5),
        'w_up': nrm(ks[21], (DEPTH, D_MODEL, FFN_HIDDEN), D_MODEL ** -0.5),
        'w_down': nrm(ks[22], (DEPTH, FFN_HIDDEN, D_MODEL), FFN_HIDDEN ** -0.5),
        'final_norm_g': 1.0 + nrm(ks[23], (D_MODEL,), 0.02),
    }


def reference(x_prompt, x_sample, state_gdn_conv, state_gdn_S, state_mlstm_C, state_mlstm_n, state_mlstm_m,
              state_ret_S, attn_norm_g, w_in, gdn_conv_w, gdn_a_log, gdn_dt_bias, gdn_norm_g, mlstm_ig_bias,
              mlstm_fg_bias, mlstm_norm_g, ret_norm_g, w_out, ffn_norm_g, w_gate, w_up, w_down, final_norm_g):
    bp = x_prompt.shape[0]
    f32 = jnp.float32
    prompt_init = (jnp.zeros((bp, CONV_W - 1, GDN_QKV), f32),
                   jnp.zeros((bp, GDN_HEADS, HEAD_DIM, HEAD_DIM), f32),
                   jnp.zeros((bp, MLSTM_HEADS, HEAD_DIM, HEAD_DIM), f32),
                   jnp.zeros((bp, MLSTM_HEADS, HEAD_DIM), f32),
                   jnp.zeros((bp, MLSTM_HEADS), f32),
                   jnp.zeros((bp, RET_HEADS, HEAD_DIM, HEAD_DIM), f32))
    xp, xs = x_prompt, x_sample
    p_states, s_states = [], []
    for l in range(DEPTH):
        params = (attn_norm_g[l], w_in[l], gdn_conv_w[l], gdn_a_log[l], gdn_dt_bias[l], gdn_norm_g[l],
                  mlstm_ig_bias[l], mlstm_fg_bias[l], mlstm_norm_g[l], ret_norm_g[l], w_out[l],
                  ffn_norm_g[l], w_gate[l], w_up[l], w_down[l])
        xp, st_p = _layer(xp, prompt_init, 0, *params)
        sample_state = (state_gdn_conv[l], state_gdn_S[l], state_mlstm_C[l], state_mlstm_n[l],
                        state_mlstm_m[l], state_ret_S[l])
        xs, st_s = _layer(xs, sample_state, PAST_LEN, *params)
        p_states.append(st_p)
        s_states.append(st_s)
    y_prompt = _rmsnorm(xp, final_norm_g)
    y_sample = _rmsnorm(xs, final_norm_g)
    p_conv, p_S, p_C, p_n, p_m, p_r = [jnp.stack([st[i] for st in p_states]) for i in range(6)]
    s_conv, s_S, s_C, s_n, s_m, s_r = [jnp.stack([st[i] for st in s_states]) for i in range(6)]
    return (y_prompt, y_sample, p_conv, p_S, p_C, p_n, p_m, p_r, s_conv, s_S, s_C, s_n, s_m, s_r)
```

```python
import functools
import math

import jax
import jax.numpy as jnp
from jax import lax
from jax.experimental import pallas as pl
from jax.experimental.pallas import tpu as pltpu

F32 = jnp.float32
BF16 = jnp.bfloat16

D_MODEL = 2048
DEPTH = 4
HD = 128
GH, MH, RH = 6, 5, 5
GW, MW, RW = GH * HD, MH * HD, RH * HD
CONV_W = 4
CHUNK = 64
FFN_HIDDEN = 5632
EPS = 1e-6
ROPE_BASE = 10000.0
RET_DECAY_OFFSET = 5
PAST_LEN = 16384

CB_GQ, CB_GK, CB_GV, CB_GG = 0, 6, 12, 18
CB_MQ, CB_MK, CB_MV, CB_MO = 24, 29, 34, 39
CB_RQ, CB_RK, CB_RV, CB_RG = 44, 49, 54, 59
CB_SMALL = 64
PROJ_W = 66 * HD
SM_BETA, SM_A, SM_I, SM_F = 0, GH, 2 * GH, 2 * GH + MH

VMEM_LIMIT = 56 * 1024 * 1024


def _cparams(sem):
    return pltpu.CompilerParams(dimension_semantics=sem, vmem_limit_bytes=VMEM_LIMIT)


def _dot(a, b):
    return jnp.dot(a.astype(BF16), b.astype(BF16), preferred_element_type=F32)


def _dot_nt(a, b):
    return lax.dot_general(a.astype(BF16), b.astype(BF16), (((1,), (1,)), ((), ())),
                           preferred_element_type=F32)


def _dot_tn(a, b):
    return lax.dot_general(a.astype(BF16), b.astype(BF16), (((0,), (0,)), ((), ())),
                           preferred_element_type=F32)


def _split(a):
    hi = a.astype(BF16)
    lo = (a - hi.astype(F32)).astype(BF16)
    return hi, lo


def _dot3(a, b):
    ah, al = _split(a)
    bh, bl = _split(b)
    d = lambda x, y: jnp.dot(x, y, preferred_element_type=F32)
    return d(ah, bh) + (d(ah, bl) + d(al, bh))


def _sigmoid(x):
    return jax.nn.sigmoid(x)


def _silu(x):
    return x * jax.nn.sigmoid(x)


def _softplus(x):
    return jnp.maximum(x, 0.0) + jnp.log1p(jnp.exp(-jnp.abs(x)))


def _lane_pick(sm, idx):
    lane = lax.broadcasted_iota(jnp.int32, sm.shape, 1)
    return jnp.sum(jnp.where(lane == idx, sm, 0.0), axis=1, keepdims=True)


class _Masks:
    def __init__(self, ch, ls):
        ii = lax.broadcasted_iota(jnp.int32, (ch, ch), 0)
        jj = lax.broadcasted_iota(jnp.int32, (ch, ch), 1)
        self.ch, self.ls = ch, ls
        self.rel = (ii - jj).astype(F32)
        self.eye = ii == jj
        if ls == ch:
            self.same = None
            self.incl = ii >= jj
            self.strict = ii > jj
            self.inclT = ii <= jj
        else:
            sh = int(math.log2(ls))
            d = (ii >> sh) - (jj >> sh)
            self.same = d == 0
            self.incl = jnp.where(self.same, ii - jj, -1) >= 0
            self.strict = jnp.where(self.same, ii - jj, -1) > 0
            self.inclT = jnp.where(self.same, jj - ii, -1) >= 0

    def to_row(self, col):
        return jnp.sum(jnp.where(self.eye, jnp.broadcast_to(col, (self.ch, self.ch)), 0.0),
                       axis=0, keepdims=True)

    def cumsum(self, col):
        ch = self.ch
        xc = jnp.broadcast_to(col, (ch, ch))
        row = self.to_row(col)
        xr = jnp.broadcast_to(row, (ch, ch))
        c_col = jnp.sum(jnp.where(self.incl, xr, 0.0), axis=1, keepdims=True)
        c_row = jnp.sum(jnp.where(self.inclT, xc, 0.0), axis=0, keepdims=True)
        if self.same is None:
            tot = jnp.sum(xr, axis=1, keepdims=True)
        else:
            tot = jnp.sum(jnp.where(self.same, xr, 0.0), axis=1, keepdims=True)
        return c_col, c_row, tot, row


def _chunk_rows(x_ref, c, ch, n_chunks):
    if n_chunks == 1:
        return x_ref[...]
    return x_ref[pl.ds(pl.multiple_of(c * ch, ch), ch), :]


def _inproj_kernel(x_ref, g_ref, w_ref, o_ref, h_ref):
    @pl.when(pl.program_id(1) == 0)
    def _():
        x = x_ref[...]
        y = x * lax.rsqrt(jnp.mean(x * x, axis=-1, keepdims=True) + EPS)
        h_ref[...] = (y * g_ref[...]).astype(BF16)

    o_ref[...] = jnp.dot(h_ref[...], w_ref[...], preferred_element_type=F32)


def _inproj(x, g_all, w_all, layer, *, tm, tn):
    rows = x.shape[0]
    return pl.pallas_call(
        _inproj_kernel,
        grid=(rows // tm, PROJ_W // tn),
        in_specs=[
            pl.BlockSpec((tm, D_MODEL), lambda i, j: (i, 0)),
            pl.BlockSpec((None, 1, D_MODEL), lambda i, j: (layer, 0, 0)),
            pl.BlockSpec((None, D_MODEL, tn), lambda i, j: (layer, 0, j)),
        ],
        out_specs=pl.BlockSpec((tm, tn), lambda i, j: (i, j)),
        out_shape=jax.ShapeDtypeStruct((rows, PROJ_W), F32),
        scratch_shapes=[pltpu.VMEM((tm, D_MODEL), BF16)],
        compiler_params=_cparams(("parallel", "arbitrary")),
        name="inproj",
    )(x, g_all, w_all)


def _conv_silu(x_ref, buf_ref, cw_ref, c, *, ch, ls, n_chunks):
    cw = cw_ref[...]

    def one(prev8, cur):
        n = cur.shape[0]
        xp = jnp.concatenate([prev8, cur], axis=0)
        acc = cur * cw[3:4]
        for w in range(CONV_W - 1):
            acc = acc + pltpu.roll(xp, (n + 8) - (5 + w), 0)[:n] * cw[w:w + 1]
        return _silu(acc)

    if n_chunks == 1:
        parts = [one(buf_ref[s], x_ref[s * ls:(s + 1) * ls, :]) for s in range(ch // ls)]
        return parts[0] if len(parts) == 1 else jnp.concatenate(parts, axis=0)
    r0 = pl.multiple_of(c * ch, ch)
    before = x_ref[pl.ds(pl.multiple_of(jnp.maximum(r0 - 8, 0), 8), 8), :]
    prev8 = jnp.where(c == 0, buf_ref[0], before)
    return one(prev8, x_ref[pl.ds(r0, ch), :])


def _gdn_kernel(alog_ref, dtb_ref, q_ref, k_ref, v_ref, gate_ref, sm_ref,
                cwq_ref, cwk_ref, cwv_ref, bq_ref, bk_ref, bv_ref, s0_ref, gn_ref,
                o_ref, s_ref, *, ch, ls, n_chunks):
    h = pl.program_id(1)
    mk = _Masks(ch, ls)
    levels = int(math.log2(ls))
    spc = ch // ls
    neg_a = -jnp.exp(jnp.full((1, 1), alog_ref[h], F32))
    dtb = dtb_ref[h]
    gn = gn_ref[...]
    eye_f = jnp.where(mk.eye, 1.0, 0.0)

    def intra(c):
        cv = functools.partial(_conv_silu, c=c, ch=ch, ls=ls, n_chunks=n_chunks)
        q = cv(q_ref, bq_ref, cwq_ref)
        k = cv(k_ref, bk_ref, cwk_ref)
        v = cv(v_ref, bv_ref, cwv_ref)
        q = q * (lax.rsqrt(jnp.sum(q * q, axis=-1, keepdims=True) + EPS) * (HD ** -0.5))
        k = k * lax.rsqrt(jnp.sum(k * k, axis=-1, keepdims=True) + EPS)
        sm = _chunk_rows(sm_ref, c, ch, n_chunks)
        beta = _sigmoid(_lane_pick(sm, SM_BETA + h))
        la = neg_a * _softplus(_lane_pick(sm, SM_A + h) + dtb)
        gc_col, gc_row, gl_col, _ = mk.cumsum(la)
        decay = jnp.where(mk.incl, jnp.exp(jnp.where(mk.incl, gc_col - gc_row, 0.0)), 0.0)
        kb = k * beta
        a = jnp.where(mk.strict, _dot_nt(kb, k) * decay, 0.0)
        x = eye_f - a
        y = a
        for _ in range(levels - 1):
            y = _dot3(y, y)
            x = x + _dot3(x, y)
        uw = _dot3(x, jnp.concatenate([v * beta, kb * jnp.exp(gc_col)], axis=1))
        u, w = uw[:, :HD], uw[:, HD:]
        qk = _dot_nt(q, k) * decay
        qg = q * jnp.exp(gc_col)
        kd = k * jnp.exp(gl_col - gc_col)
        eg = jnp.exp(gl_col)
        return u, w, qk, qg, kd, eg

    def finish(o, c):
        gate = _chunk_rows(gate_ref, c, ch, n_chunks)
        y = o * lax.rsqrt(jnp.mean(o * o, axis=-1, keepdims=True) + EPS) * gn
        return y * _silu(gate)

    if n_chunks > 1:
        def body(c, s):
            u, w, qk, qg, kd, eg = intra(c)
            sb = s.astype(BF16)
            v_new = u - _dot(w, sb)
            o = _dot(qg, sb) + _dot(qk, v_new)
            o_ref[pl.ds(pl.multiple_of(c * ch, ch), ch), :] = finish(o, c)
            return eg[0:1, :] * s + _dot_tn(kd, v_new)

        s_ref[0, 0] = lax.fori_loop(0, n_chunks, body, s0_ref[0, 0])
    else:
        u, w, qk, qg, kd, eg = intra(0)
        vn, o1 = [], []
        for s in range(spc):
            sl = slice(s * ls, (s + 1) * ls)
            r = _dot(jnp.concatenate([w[sl], qg[sl]], axis=0), s0_ref[s, 0])
            vn.append(u[sl] - r[:ls])
            o1.append(r[ls:])
        v_new = jnp.concatenate(vn, axis=0)
        o = jnp.concatenate(o1, axis=0) + _dot(qk, v_new)
        o_ref[...] = finish(o, 0)
        rows = lax.broadcasted_iota(jnp.int32, (ch, HD), 0)
        vb = v_new.astype(BF16)
        for s in range(spc):
            kd_s = jnp.where((rows >= s * ls) & (rows < (s + 1) * ls), kd, 0.0)
            s_ref[s, 0] = eg[s * ls:s * ls + 1, :] * s0_ref[s, 0] + _dot_tn(kd_s, vb)


def _gdn(proj, buf8, s0, cw8, alog, dtb, gnorm, layer, s0_layer, *, n_seq, t, ch, ls):
    n_chunks = t // ls if ls == ch else 1
    nb = 1 if ls == ch else ch // ls
    rows = nb * t
    grid = (n_seq // nb, GH)
    col = lambda cb: pl.BlockSpec((rows, HD), lambda i, h: (i, cb + h))
    cwb = lambda cb: pl.BlockSpec((None, 8, HD), lambda i, h: (layer, 0, cb + h))
    bufb = lambda cb: pl.BlockSpec((nb, 8, HD), lambda i, h: (i, 0, cb + h))
    smem = pl.BlockSpec(memory_space=pltpu.SMEM)
    if s0_layer is None:
        s0_spec = pl.BlockSpec((nb, 1, HD, HD), lambda i, h: (i, h, 0, 0))
    else:
        s0_spec = pl.BlockSpec((None, nb, 1, HD, HD), lambda i, h: (s0_layer, i, h, 0, 0))
    return pl.pallas_call(
        functools.partial(_gdn_kernel, ch=ch, ls=ls, n_chunks=n_chunks),
        grid=grid,
        in_specs=[smem, smem, col(CB_GQ), col(CB_GK), col(CB_GV), col(CB_GG),
                  pl.BlockSpec((rows, HD), lambda i, h: (i, CB_SMALL)),
                  cwb(0), cwb(GH), cwb(2 * GH), bufb(0), bufb(GH), bufb(2 * GH),
                  s0_spec,
                  pl.BlockSpec((None, 1, HD), lambda i, h: (layer, 0, 0))],
        out_specs=[pl.BlockSpec((rows, HD), lambda i, h: (i, h)),
                   pl.BlockSpec((nb, 1, HD, HD), lambda i, h: (i, h, 0, 0))],
        out_shape=[jax.ShapeDtypeStruct((n_seq * t, GW), F32),
                   jax.ShapeDtypeStruct((n_seq, GH, HD, HD), F32)],
        compiler_params=_cparams(("parallel", "arbitrary")),
        name="gdn",
    )(alog, dtb, proj, proj, proj, proj, proj, cw8, cw8, cw8, buf8, buf8, buf8, s0, gnorm)


def _mlstm_kernel(igb_ref, fgb_ref, q_ref, k_ref, v_ref, og_ref, sm_ref,
                  c0_ref, n0_ref, m0_ref, mg_ref,
                  o_ref, c_ref, n_ref, m_ref, *, ch, ls, n_chunks):
    h = pl.program_id(1)
    mk = _Masks(ch, ls)
    spc = ch // ls
    igb = igb_ref[h]
    fgb = fgb_ref[h]
    mg = mg_ref[...]
    neg_inf = F32(-jnp.inf)

    def gates(c):
        sm = _chunk_rows(sm_ref, c, ch, n_chunks)
        ig = _lane_pick(sm, SM_I + h) + igb
        lf = -_softplus(-(_lane_pick(sm, SM_F + h) + fgb))
        b_col, b_row, b_tot, _ = mk.cumsum(lf)
        ig_row = mk.to_row(ig)
        logd = jnp.where(mk.incl, b_col - b_row + ig_row, neg_inf)
        rowmax = jnp.max(logd, axis=1, keepdims=True)
        return ig, b_col, b_tot, logd, rowmax

    def attend(c, m_col, cq_fn, nq_fn):
        q = _chunk_rows(q_ref, c, ch, n_chunks)
        k = _chunk_rows(k_ref, c, ch, n_chunks) * (HD ** -0.5)
        v = _chunk_rows(v_ref, c, ch, n_chunks)
        ig, b_col, b_tot, logd, rowmax = gates(c)
        inter = b_col + m_col
        m_t = jnp.maximum(inter, rowmax)
        s = _dot_nt(q, k) * jnp.exp(logd - m_t)
        w_inter = jnp.exp(inter - m_t)
        num = _dot(s, v) + w_inter * cq_fn(q)
        den = jnp.sum(s, axis=1, keepdims=True) + w_inter * nq_fn(q)
        hh = num / jnp.maximum(jnp.abs(den), jnp.exp(-m_t))
        og = _chunk_rows(og_ref, c, ch, n_chunks)
        y = hh * lax.rsqrt(jnp.mean(hh * hh, axis=-1, keepdims=True) + EPS) * mg
        return y * _sigmoid(og), k, v, ig, b_col, b_tot, inter, m_t

    if n_chunks > 1:
        def body(c, carry):
            cm, n, m = carry
            cb = cm.astype(BF16)
            y, k, v, ig, b_col, b_tot, inter, m_t = attend(
                c, m, lambda q: _dot(q, cb), lambda q: jnp.sum(q * n, axis=1, keepdims=True))
            r0 = pl.multiple_of(c * ch, ch)
            o_ref[pl.ds(r0, ch), :] = y
            m_ref[pl.ds(r0, ch), :] = jnp.broadcast_to(m_t, (ch, HD))
            m_new = m_t[ch - 1:ch, :]
            w_key = jnp.exp(b_tot - b_col + ig - m_new)
            dec = jnp.exp(inter[ch - 1:ch, :] - m_new)
            kw = k * w_key
            return (dec * cm + _dot_tn(kw, v), dec * n + jnp.sum(kw, axis=0, keepdims=True), m_new)

        cm, n, _ = lax.fori_loop(0, n_chunks, body, (c0_ref[0, 0], n0_ref[0, 0], m0_ref[0, 0][:, 0:1]))
        c_ref[0, 0] = cm
        n_ref[0, 0] = n
    else:
        rows1 = lax.broadcasted_iota(jnp.int32, (ch, 1), 0)

        def per_seq_col(vals):
            out = jnp.zeros((ch, 1), F32)
            for s in range(spc):
                out = jnp.where((rows1 >= s * ls) & (rows1 < (s + 1) * ls), vals[s], out)
            return out

        m_col = per_seq_col([m0_ref[s, 0][:, 0:1] for s in range(spc)])

        def cq_fn(q):
            return jnp.concatenate(
                [_dot(q[s * ls:(s + 1) * ls], c0_ref[s, 0]) for s in range(spc)], axis=0)

        def nq_fn(q):
            return jnp.concatenate(
                [jnp.sum(q[s * ls:(s + 1) * ls] * n0_ref[s, 0], axis=1, keepdims=True) for s in range(spc)],
                axis=0)

        y, k, v, ig, b_col, b_tot, inter, m_t = attend(0, m_col, cq_fn, nq_fn)
        o_ref[...] = y
        m_ref[...] = jnp.broadcast_to(m_t, (ch, HD))
        m_new = per_seq_col([m_t[(s + 1) * ls - 1:(s + 1) * ls, :] for s in range(spc)])
        dec_col = jnp.exp(per_seq_col([inter[(s + 1) * ls - 1:(s + 1) * ls, :] for s in range(spc)]) - m_new)
        kw = k * jnp.exp(b_tot - b_col + ig - m_new)
        rows = lax.broadcasted_iota(jnp.int32, (ch, HD), 0)
        vb = v.astype(BF16)
        for s in range(spc):
            inseq = (rows >= s * ls) & (rows < (s + 1) * ls)
            kw_s = jnp.where(inseq, kw, 0.0)
            dec = dec_col[s * ls:s * ls + 1, :]
            c_ref[s, 0] = dec * c0_ref[s, 0] + _dot_tn(kw_s, vb)
            n_ref[s, 0] = dec * n0_ref[s, 0] + jnp.sum(kw_s, axis=0, keepdims=True)


def _mlstm(proj, c0, n0, m0, igb, fgb, mnorm, layer, st_layer, *, n_seq, t, ch, ls):
    n_chunks = t // ls if ls == ch else 1
    nb = 1 if ls == ch else ch // ls
    rows = nb * t
    grid = (n_seq // nb, MH)
    col = lambda cb: pl.BlockSpec((rows, HD), lambda i, h: (i, cb + h))
    smem = pl.BlockSpec(memory_space=pltpu.SMEM)

    def st(shape):
        if st_layer is None:
            return pl.BlockSpec((nb, 1) + shape, lambda i, h: (i, h, 0, 0))
        return pl.BlockSpec((None, nb, 1) + shape, lambda i, h: (st_layer, i, h, 0, 0))

    return pl.pallas_call(
        functools.partial(_mlstm_kernel, ch=ch, ls=ls, n_chunks=n_chunks),
        grid=grid,
        in_specs=[smem, smem, col(CB_MQ), col(CB_MK), col(CB_MV), col(CB_MO),
                  pl.BlockSpec((rows, HD), lambda i, h: (i, CB_SMALL)),
                  st((HD, HD)), st((1, HD)), st((1, HD)),
                  pl.BlockSpec((None, None, 1, HD), lambda i, h: (layer, h, 0, 0))],
        out_specs=[pl.BlockSpec((rows, HD), lambda i, h: (i, h)),
                   pl.BlockSpec((nb, 1, HD, HD), lambda i, h: (i, h, 0, 0)),
                   pl.BlockSpec((nb, 1, 1, HD), lambda i, h: (i, h, 0, 0)),
                   pl.BlockSpec((rows, HD), lambda i, h: (i, h))],
        out_shape=[jax.ShapeDtypeStruct((n_seq * t, MW), F32),
                   jax.ShapeDtypeStruct((n_seq, MH, HD, HD), F32),
                   jax.ShapeDtypeStruct((n_seq, MH, 1, HD), F32),
                   jax.ShapeDtypeStruct((n_seq * t, MW), F32)],
        compiler_params=_cparams(("parallel", "arbitrary")),
        name="mlstm",
    )(igb, fgb, proj, proj, proj, proj, proj, c0, n0, m0, mnorm)


def _ret_kernel(lg_ref, q_ref, k_ref, v_ref, g_ref, cos_ref, sin_ref, s0_ref, rg_ref,
                o_ref, s_ref, *, ch, ls, n_chunks):
    h = pl.program_id(1)
    mk = _Masks(ch, ls)
    spc = ch // ls
    lg = lg_ref[h]
    rg = rg_ref[...]
    pos = (lax.broadcasted_iota(jnp.int32, (ch, 1), 0) & (ls - 1)).astype(F32)
    inc = mk.incl
    dmat = jnp.where(inc, jnp.exp(jnp.where(inc, mk.rel, 0.0) * lg), 0.0)
    q_decay = jnp.exp((pos + 1.0) * lg)
    k_decay = jnp.exp((ls - 1.0 - pos) * lg)
    cd = jnp.exp(jnp.full((1, 1), ls * lg, F32))

    def rope(x, cos, sin):
        return x * cos + pltpu.roll(x, HD // 2, 1) * sin

    def attend(c):
        cos = _chunk_rows(cos_ref, c, ch, n_chunks)
        sin = _chunk_rows(sin_ref, c, ch, n_chunks)
        q = rope(_chunk_rows(q_ref, c, ch, n_chunks), cos, sin)
        k = rope(_chunk_rows(k_ref, c, ch, n_chunks), cos, sin) * (HD ** -0.5)
        v = _chunk_rows(v_ref, c, ch, n_chunks)
        s = _dot_nt(q, k) * dmat
        return _dot(s, v), q * q_decay, k * k_decay, v

    def finish(o, c):
        g = _chunk_rows(g_ref, c, ch, n_chunks)
        mu = jnp.mean(o, axis=-1, keepdims=True)
        var = jnp.mean(jnp.square(o - mu), axis=-1, keepdims=True)
        return (o - mu) * lax.rsqrt(var + EPS) * rg * _silu(g)

    if n_chunks > 1:
        def body(c, s):
            o, qd, kd, v = attend(c)
            o = o + _dot(qd, s)
            o_ref[pl.ds(pl.multiple_of(c * ch, ch), ch), :] = finish(o, c)
            return cd * s + _dot_tn(kd, v)

        s_ref[0, 0] = lax.fori_loop(0, n_chunks, body, s0_ref[0, 0])
    else:
        o, qd, kd, v = attend(0)
        o = o + jnp.concatenate(
            [_dot(qd[s * ls:(s + 1) * ls], s0_ref[s, 0]) for s in range(spc)], axis=0)
        o_ref[...] = finish(o, 0)
        rows = lax.broadcasted_iota(jnp.int32, (ch, HD), 0)
        vb = v.astype(BF16)
        for s in range(spc):
            kd_s = jnp.where((rows >= s * ls) & (rows < (s + 1) * ls), kd, 0.0)
            s_ref[s, 0] = cd * s0_ref[s, 0] + _dot_tn(kd_s, vb)


def _ret(proj, s0, cos, sin, lg, rnorm, layer, s0_layer, *, n_seq, t, ch, ls):
    n_chunks = t // ls if ls == ch else 1
    nb = 1 if ls == ch else ch // ls
    rows = nb * t
    grid = (n_seq // nb, RH)
    col = lambda cb: pl.BlockSpec((rows, HD), lambda i, h: (i, cb + h))
    tab = pl.BlockSpec((rows, HD), lambda i, h: (0, 0))
    if s0_layer is None:
        s0_spec = pl.BlockSpec((nb, 1, HD, HD), lambda i, h: (i, h, 0, 0))
    else:
        s0_spec = pl.BlockSpec((None, nb, 1, HD, HD), lambda i, h: (s0_layer, i, h, 0, 0))
    return pl.pallas_call(
        functools.partial(_ret_kernel, ch=ch, ls=ls, n_chunks=n_chunks),
        grid=grid,
        in_specs=[pl.BlockSpec(memory_space=pltpu.SMEM),
                  col(CB_RQ), col(CB_RK), col(CB_RV), col(CB_RG), tab, tab, s0_spec,
                  pl.BlockSpec((None, None, 1, HD), lambda i, h: (layer, h, 0, 0))],
        out_specs=[pl.BlockSpec((rows, HD), lambda i, h: (i, h)),
                   pl.BlockSpec((nb, 1, HD, HD), lambda i, h: (i, h, 0, 0))],
        out_shape=[jax.ShapeDtypeStruct((n_seq * t, RW), F32),
                   jax.ShapeDtypeStruct((n_seq, RH, HD, HD), F32)],
        compiler_params=_cparams(("parallel", "arbitrary")),
        name="retention",
    )(lg, proj, proj, proj, proj, cos, sin, s0, rnorm)


def _outproj_kernel(x_ref, go_ref, mh_ref, ro_ref, wg_ref, wm_ref, wr_ref, o_ref):
    acc = jnp.dot(go_ref[...].astype(BF16), wg_ref[...], preferred_element_type=F32)
    acc = acc + jnp.dot(mh_ref[...].astype(BF16), wm_ref[...], preferred_element_type=F32)
    acc = acc + jnp.dot(ro_ref[...].astype(BF16), wr_ref[...], preferred_element_type=F32)
    o_ref[...] = x_ref[...] + acc


def _outproj(x, go, mh, ro, wg, wm, wr, layer, *, tm, tn):
    rows = x.shape[0]
    return pl.pallas_call(
        _outproj_kernel,
        grid=(rows // tm, D_MODEL // tn),
        in_specs=[pl.BlockSpec((tm, tn), lambda i, j: (i, j)),
                  pl.BlockSpec((tm, GW), lambda i, j: (i, 0)),
                  pl.BlockSpec((tm, MW), lambda i, j: (i, 0)),
                  pl.BlockSpec((tm, RW), lambda i, j: (i, 0)),
                  pl.BlockSpec((None, GW, tn), lambda i, j: (layer, 0, j)),
                  pl.BlockSpec((None, MW, tn), lambda i, j: (layer, 0, j)),
                  pl.BlockSpec((None, RW, tn), lambda i, j: (layer, 0, j))],
        out_specs=pl.BlockSpec((tm, tn), lambda i, j: (i, j)),
        out_shape=jax.ShapeDtypeStruct((rows, D_MODEL), F32),
        compiler_params=_cparams(("parallel", "arbitrary")),
        name="outproj",
    )(x, go, mh, ro, wg, wm, wr)


def _ffn_kernel(x_ref, g_ref, wg_ref, wu_ref, wd_ref, o_ref, h_ref):
    j = pl.program_id(1)

    @pl.when(j == 0)
    def _():
        x = x_ref[...]
        y = x * lax.rsqrt(jnp.mean(x * x, axis=-1, keepdims=True) + EPS)
        h_ref[...] = (y * g_ref[...]).astype(BF16)
        o_ref[...] = x

    hb = h_ref[...]
    a = jnp.dot(hb, wg_ref[...], preferred_element_type=F32)
    b = jnp.dot(hb, wu_ref[...], preferred_element_type=F32)
    ff = (_silu(a) * b).astype(BF16)
    o_ref[...] += jnp.dot(ff, wd_ref[...], preferred_element_type=F32)


def _ffn(x, g_all, wg, wu, wd, layer, *, tm, th):
    rows = x.shape[0]
    return pl.pallas_call(
        _ffn_kernel,
        grid=(rows // tm, FFN_HIDDEN // th),
        in_specs=[pl.BlockSpec((tm, D_MODEL), lambda i, j: (i, 0)),
                  pl.BlockSpec((None, 1, D_MODEL), lambda i, j: (layer, 0, 0)),
                  pl.BlockSpec((None, D_MODEL, th), lambda i, j: (layer, 0, j)),
                  pl.BlockSpec((None, D_MODEL, th), lambda i, j: (layer, 0, j)),
                  pl.BlockSpec((None, th, D_MODEL), lambda i, j: (layer, j, 0))],
        out_specs=pl.BlockSpec((tm, D_MODEL), lambda i, j: (i, 0)),
        out_shape=jax.ShapeDtypeStruct((rows, D_MODEL), F32),
        scratch_shapes=[pltpu.VMEM((tm, D_MODEL), BF16)],
        compiler_params=_cparams(("parallel", "arbitrary")),
        name="ffn",
    )(x, g_all, wg, wu, wd)


def _norm_kernel(x_ref, g_ref, o_ref):
    x = x_ref[...]
    o_ref[...] = x * lax.rsqrt(jnp.mean(x * x, axis=-1, keepdims=True) + EPS) * g_ref[...]


def _final_norm(x, g, *, tm):
    rows = x.shape[0]
    return pl.pallas_call(
        _norm_kernel,
        grid=(rows // tm,),
        in_specs=[pl.BlockSpec((tm, D_MODEL), lambda i: (i, 0)),
                  pl.BlockSpec((1, D_MODEL), lambda i: (0, 0))],
        out_specs=pl.BlockSpec((tm, D_MODEL), lambda i: (i, 0)),
        out_shape=jax.ShapeDtypeStruct((rows, D_MODEL), F32),
        compiler_params=_cparams(("parallel",)),
        name="final_norm",
    )(x, g)


def _relayout_w_in(w_in):
    sizes = (3 * GW, GW, GH, GH, 3 * MW, MW, MH, MH, 3 * RW)
    gq, gg, gb, ga, mq, mo, mi, mf, rq, rg = jnp.split(
        w_in, [sum(sizes[:n + 1]) for n in range(len(sizes))], axis=-1)
    pad = jnp.zeros(w_in.shape[:-1] + (2 * HD - (2 * GH + 2 * MH),), w_in.dtype)
    return jnp.concatenate([gq, gg, mq, mo, rq, rg, gb, ga, mi, mf, pad], axis=-1).astype(BF16)


def _rope_tables(pos0, t):
    half = HD // 2
    inv = ROPE_BASE ** (-jnp.arange(half, dtype=F32) / half)
    ang = (pos0 + jnp.arange(t)).astype(F32)[:, None] * inv[None, :]
    cos, sin = jnp.cos(ang), jnp.sin(ang)
    return jnp.concatenate([cos, cos], axis=-1), jnp.concatenate([-sin, sin], axis=-1)


def _trunk(x, states, state_layered, pos0, prm, *, n_seq, t, ch, ls, tm):
    conv8, g_s, m_c, m_n, m_m, r_s = states
    cos, sin = _rope_tables(pos0, t)
    reps = (ch // ls) if ls != ch else 1
    cos, sin = jnp.tile(cos, (reps, 1)), jnp.tile(sin, (reps, 1))
    outs = []
    for l in range(DEPTH):
        sl = l if state_layered else None
        proj = _inproj(x, prm["attn_g"], prm["w_in"], l, tm=tm, tn=768)
        buf8 = conv8[l] if state_layered else conv8
        go, g_s_new = _gdn(proj, buf8, g_s, prm["conv_w8"], prm["a_log"][l], prm["dt_bias"][l], prm["gdn_g"],
                           l, sl, n_seq=n_seq, t=t, ch=ch, ls=ls)
        mh, m_c_new, m_n_new, m_rows = _mlstm(proj, m_c, m_n, m_m, prm["ig_b"][l], prm["fg_b"][l], prm["mlstm_g"],
                                              l, sl, n_seq=n_seq, t=t, ch=ch, ls=ls)
        ro, r_s_new = _ret(proj, r_s, cos, sin, prm["log_gamma"], prm["ret_g"], l, sl,
                           n_seq=n_seq, t=t, ch=ch, ls=ls)
        x = _outproj(x, go, mh, ro, prm["wo_g"], prm["wo_m"], prm["wo_r"], l, tm=tm, tn=1024)
        x = _ffn(x, prm["ffn_g"], prm["w_gate"], prm["w_up"], prm["w_down"], l, tm=tm, th=512)
        p3 = proj.reshape(n_seq, t, PROJ_W)
        outs.append((p3[:, t - (CONV_W - 1):, :3 * GW], g_s_new, m_c_new, m_n_new[:, :, 0, :],
                     m_rows.reshape(n_seq, t, MH, HD)[:, t - 1, :, 0], r_s_new))
    y = _final_norm(x, prm["final_g"], tm=tm)
    return y, [jnp.stack([o[i] for o in outs]) for i in range(6)]


def kernel(x_prompt, x_sample, state_gdn_conv, state_gdn_S, state_mlstm_C, state_mlstm_n, state_mlstm_m,
           state_ret_S, attn_norm_g, w_in, gdn_conv_w, gdn_a_log, gdn_dt_bias, gdn_norm_g, mlstm_ig_bias,
           mlstm_fg_bias, mlstm_norm_g, ret_norm_g, w_out, ffn_norm_g, w_gate, w_up, w_down, final_norm_g):
    bp, tp, _ = x_prompt.shape
    bs, ts, _ = x_sample.shape
    prm = {
        "attn_g": attn_norm_g[:, None, :],
        "w_in": _relayout_w_in(w_in),
        "conv_w8": jnp.pad(gdn_conv_w, ((0, 0), (0, 8 - CONV_W), (0, 0))),
        "a_log": gdn_a_log, "dt_bias": gdn_dt_bias,
        "gdn_g": gdn_norm_g[:, None, :],
        "ig_b": mlstm_ig_bias, "fg_b": mlstm_fg_bias,
        "mlstm_g": mlstm_norm_g[:, :, None, :],
        "ret_g": ret_norm_g[:, :, None, :],
        "log_gamma": jnp.log1p(-jnp.exp2(-(RET_DECAY_OFFSET + jnp.arange(RH, dtype=F32)))),
        "wo_g": w_out[:, :GW].astype(BF16),
        "wo_m": w_out[:, GW:GW + MW].astype(BF16),
        "wo_r": w_out[:, GW + MW:].astype(BF16),
        "ffn_g": ffn_norm_g[:, None, :],
        "w_gate": w_gate.astype(BF16), "w_up": w_up.astype(BF16), "w_down": w_down.astype(BF16),
        "final_g": final_norm_g[None, :],
    }
    zeros = lambda *s: jnp.zeros(s, F32)
    p_states = (zeros(bp, 8, 3 * GW), zeros(bp, GH, HD, HD), zeros(bp, MH, HD, HD),
                zeros(bp, MH, 1, HD), zeros(bp, MH, 1, HD), zeros(bp, RH, HD, HD))
    s_states = (jnp.pad(state_gdn_conv, ((0, 0), (0, 0), (8 - (CONV_W - 1), 0), (0, 0))),
                state_gdn_S, state_mlstm_C, state_mlstm_n[:, :, :, None, :],
                jnp.broadcast_to(state_mlstm_m[:, :, :, None, None], state_mlstm_m.shape + (1, HD)),
                state_ret_S)
    lp = math.gcd(tp, CHUNK)
    ls = math.gcd(ts, CHUNK)
    yp, pst = _trunk(x_prompt.reshape(bp * tp, D_MODEL), p_states, False, 0, prm,
                     n_seq=bp, t=tp, ch=lp, ls=lp, tm=1024)
    ys, sst = _trunk(x_sample.reshape(bs * ts, D_MODEL), s_states, True, PAST_LEN, prm,
                     n_seq=bs, t=ts, ch=HD, ls=ls, tm=1024)
    return (yp.reshape(bp, tp, D_MODEL), ys.reshape(bs, ts, D_MODEL), *pst, *sst)
```

```python
import functools
import math

import jax
import jax.numpy as jnp
from jax import lax
from jax.experimental import pallas as pl
from jax.experimental.pallas import tpu as pltpu

F32 = jnp.float32
BF16 = jnp.bfloat16

D_MODEL = 2048
DEPTH = 4
HD = 128
GH, MH, RH = 6, 5, 5
GW, MW, RW = GH * HD, MH * HD, RH * HD
CONV_W = 4
CHUNK = 64
FFN_HIDDEN = 5632
EPS = 1e-6
ROPE_BASE = 10000.0
RET_DECAY_OFFSET = 5
PAST_LEN = 16384

CB_MQ, CB_MK, CB_MV, CB_MO, CB_RQ, CB_RK, CB_RV, CB_RG = range(8)
CB_SMALL = 8 * MW // HD
CB_GQ, CB_GK, CB_GV, CB_GG = (8 * MW + 2 * HD) // GW + 0, (8 * MW + 2 * HD) // GW + 1, \
    (8 * MW + 2 * HD) // GW + 2, (8 * MW + 2 * HD) // GW + 3
GDN_COL0 = CB_GQ * GW
PROJ_W = GDN_COL0 + 4 * GW
SM_BETA, SM_A, SM_I, SM_F = 0, GH, 2 * GH, 2 * GH + MH

VMEM_LIMIT = 56 * 1024 * 1024
PROMPT_ROWS = 512


def _cparams(sem):
    return pltpu.CompilerParams(dimension_semantics=sem, vmem_limit_bytes=VMEM_LIMIT)


def _dot(a, b):
    return jnp.dot(a.astype(BF16), b.astype(BF16), preferred_element_type=F32)


def _dot_nt(a, b):
    return lax.dot_general(a.astype(BF16), b.astype(BF16), (((1,), (1,)), ((), ())),
                           preferred_element_type=F32)


def _dot_tn(a, b):
    return lax.dot_general(a.astype(BF16), b.astype(BF16), (((0,), (0,)), ((), ())),
                           preferred_element_type=F32)


def _split(a):
    hi = a.astype(BF16)
    lo = (a - hi.astype(F32)).astype(BF16)
    return hi, lo


def _dot3(a, b):
    ah, al = _split(a)
    bh, bl = _split(b)
    d = lambda x, y: jnp.dot(x, y, preferred_element_type=F32)
    return d(ah, bh) + (d(ah, bl) + d(al, bh))


def _sigmoid(x):
    return jax.nn.sigmoid(x)


def _silu(x):
    return x * jax.nn.sigmoid(x)


def _softplus(x):
    return jnp.maximum(x, 0.0) + jnp.log1p(jnp.exp(-jnp.abs(x)))


class _Masks:
    def __init__(self, ch, ls):
        ii = lax.broadcasted_iota(jnp.int32, (ch, ch), 0)
        jj = lax.broadcasted_iota(jnp.int32, (ch, ch), 1)
        self.ch, self.ls = ch, ls
        self.rel = (ii - jj).astype(F32)
        self.eye = ii == jj
        if ls == ch:
            self.same = None
            self.incl = ii >= jj
            self.strict = ii > jj
            self.inclT = ii <= jj
        else:
            sh = int(math.log2(ls))
            self.same = (ii >> sh) == (jj >> sh)
            self.incl = jnp.where(self.same, ii - jj, -1) >= 0
            self.strict = jnp.where(self.same, ii - jj, -1) > 0
            self.inclT = jnp.where(self.same, jj - ii, -1) >= 0

    def to_row(self, col):
        return jnp.sum(jnp.where(self.eye, jnp.broadcast_to(col, (self.ch, self.ch)), 0.0),
                       axis=0, keepdims=True)

    def cumsum(self, col):
        ch = self.ch
        xc = jnp.broadcast_to(col, (ch, ch))
        xr = jnp.broadcast_to(self.to_row(col), (ch, ch))
        c_col = jnp.sum(jnp.where(self.incl, xr, 0.0), axis=1, keepdims=True)
        c_row = jnp.sum(jnp.where(self.inclT, xc, 0.0), axis=0, keepdims=True)
        if self.same is None:
            tot = jnp.sum(xr, axis=1, keepdims=True)
        else:
            tot = jnp.sum(jnp.where(self.same, xr, 0.0), axis=1, keepdims=True)
        return c_col, c_row, tot


def _rows(x_ref, c, ch, cols=slice(None)):
    if c is None:
        return x_ref[:, cols]
    return x_ref[pl.ds(pl.multiple_of(c * ch, ch), ch), cols]


def _seq_mask(ch, ls, s):
    rows = lax.broadcasted_iota(jnp.int32, (ch, HD), 0)
    return (rows >= s * ls) & (rows < (s + 1) * ls)


def _chunk_loop(nck, body):
    if nck == 1:
        body(None)
    else:
        lax.fori_loop(0, nck, lambda c, carry: (body(c), carry)[1], 0)


def _inproj_kernel(x_ref, g_ref, w_ref, o_ref, h_ref):
    @pl.when(pl.program_id(1) == 0)
    def _():
        x = x_ref[...]
        y = x * lax.rsqrt(jnp.mean(x * x, axis=-1, keepdims=True) + EPS)
        h_ref[...] = (y * g_ref[...]).astype(BF16)

    o_ref[...] = jnp.dot(h_ref[...], w_ref[...], preferred_element_type=F32)


def _inproj(x, g_all, w_all, layer, *, tm, tn):
    rows = x.shape[0]
    return pl.pallas_call(
        _inproj_kernel,
        grid=(rows // tm, PROJ_W // tn),
        in_specs=[
            pl.BlockSpec((tm, D_MODEL), lambda i, j: (i, 0)),
            pl.BlockSpec((None, 1, D_MODEL), lambda i, j: (layer, 0, 0)),
            pl.BlockSpec((None, D_MODEL, tn), lambda i, j: (layer, 0, j)),
        ],
        out_specs=pl.BlockSpec((tm, tn), lambda i, j: (i, j)),
        out_shape=jax.ShapeDtypeStruct((rows, PROJ_W), F32),
        scratch_shapes=[pltpu.VMEM((tm, D_MODEL), BF16)],
        compiler_params=_cparams(("parallel", "arbitrary")),
        name="inproj",
    )(x, g_all, w_all)


class _Geom:
    def __init__(self, n_seq, t, ch, ls):
        self.n_seq, self.t, self.ch, self.ls = n_seq, t, ch, ls
        self.carry = ls == ch
        if self.carry:
            self.rows = min(PROMPT_ROWS, t)
            self.nb = 1
            self.grid = (n_seq, t // self.rows)
        else:
            self.rows = ch
            self.nb = ch // ls
            self.grid = (n_seq // self.nb, 1)
        self.nck = self.rows // ch
        nr = self.grid[1]
        self.row_spec = lambda width, cb: pl.BlockSpec((self.rows, width), lambda i, r: (i * nr + r, cb))

    def state_in(self, layer, tail):
        zeros = (0,) * len(tail)
        if layer is None:
            return pl.BlockSpec((self.nb,) + tail, lambda i, r: (i,) + zeros)
        return pl.BlockSpec((None, self.nb) + tail, lambda i, r: (layer, i) + zeros)

    def state_out(self, layer, tail):
        zeros = (0,) * len(tail)
        return pl.BlockSpec((None, self.nb) + tail, lambda i, r: (layer, i) + zeros)

    def stack_shape(self, tail):
        return jax.ShapeDtypeStruct((DEPTH, self.n_seq) + tail, F32)


def _mixer_call(kernel_fn, geom, args, in_specs, out_specs, out_shape, scratch, prev, name):
    n_in = len(args)
    prev = [] if prev is None else list(prev)

    def body(*refs):
        return kernel_fn(*refs[:n_in], *refs[n_in + len(prev):])

    return pl.pallas_call(
        body,
        grid=geom.grid,
        in_specs=list(in_specs) + [pl.BlockSpec(memory_space=pl.ANY)] * len(prev),
        out_specs=out_specs,
        out_shape=out_shape,
        scratch_shapes=scratch,
        input_output_aliases={n_in + k: 1 + k for k in range(len(prev))},
        compiler_params=_cparams(("parallel", "arbitrary")),
        name=name,
    )(*args, *prev)


def _gdn_kernel(alog_ref, dtb_ref, q_ref, k_ref, v_ref, gate_ref, sm_ref, cw_ref, buf_ref, s0_ref, gn_ref,
                o_ref, s_ref, *scratch, ch, ls, nck, carry):
    r = pl.program_id(1)
    mk = _Masks(ch, ls)
    levels = int(math.log2(ls))
    spc = ch // ls
    gn = gn_ref[...]
    eye_f = jnp.where(mk.eye, 1.0, 0.0)
    if carry:
        s_scr, tail_scr = scratch

        @pl.when(r == 0)
        def _():
            s_scr[...] = s0_ref[0]
            tail_scr[...] = buf_ref[0]

    def conv(x_ref, a, h, c):
        cols = slice(h * HD, (h + 1) * HD)
        ccols = slice(a * GW + h * HD, a * GW + (h + 1) * HD)
        cw = cw_ref[:, ccols]

        def one(prev8, cur):
            n = cur.shape[0]
            xp = jnp.concatenate([prev8, cur], axis=0)
            acc = cur * cw[3:4]
            for w in range(CONV_W - 1):
                acc = acc + pltpu.roll(xp, (n + 8) - (5 + w), 0)[:n] * cw[w:w + 1]
            return _silu(acc)

        if not carry:
            return jnp.concatenate(
                [one(buf_ref[s, :, ccols], x_ref[s * ls:(s + 1) * ls, cols]) for s in range(spc)], axis=0)
        if c is None:
            return one(tail_scr[:, ccols], x_ref[:, cols])
        r0 = pl.multiple_of(c * ch, ch)
        before = x_ref[pl.ds(pl.multiple_of(jnp.maximum(r0 - 8, 0), 8), 8), cols]
        return one(jnp.where(c == 0, tail_scr[:, ccols], before), x_ref[pl.ds(r0, ch), cols])

    def chunk(c):
        sm = _rows(sm_ref, c, ch)
        hs = range(GH)
        st = []
        for h in hs:
            q = conv(q_ref, 0, h, c)
            k = conv(k_ref, 1, h, c)
            v = conv(v_ref, 2, h, c)
            q = q * (lax.rsqrt(jnp.sum(q * q, axis=-1, keepdims=True) + EPS) * (HD ** -0.5))
            k = k * lax.rsqrt(jnp.sum(k * k, axis=-1, keepdims=True) + EPS)
            beta = _sigmoid(sm[:, SM_BETA + h:SM_BETA + h + 1])
            neg_a = -jnp.exp(jnp.full((1, 1), alog_ref[h], F32))
            la = neg_a * _softplus(sm[:, SM_A + h:SM_A + h + 1] + dtb_ref[h])
            gc_col, gc_row, gl_col = mk.cumsum(la)
            decay = jnp.where(mk.incl, jnp.exp(jnp.where(mk.incl, gc_col - gc_row, 0.0)), 0.0)
            kb = k * beta
            st.append(dict(q=q, k=k, kb=kb, decay=decay, gc_col=gc_col, gl_col=gl_col,
                           rhs=jnp.concatenate([v * beta, kb * jnp.exp(gc_col)], axis=1)))
        for d in st:
            d["y"] = jnp.where(mk.strict, _dot_nt(d["kb"], d["k"]) * d["decay"], 0.0)
            d["x"] = eye_f - d["y"]
        for _ in range(levels - 1):
            for d in st:
                d["y"] = _dot3(d["y"], d["y"])
            for d in st:
                d["x"] = d["x"] + _dot3(d["x"], d["y"])
        for d in st:
            d["uw"] = _dot3(d["x"], d["rhs"])
            d["qk"] = _dot_nt(d["q"], d["k"]) * d["decay"]
        for h in hs:
            d = st[h]
            d["u"], w = d["uw"][:, :HD], d["uw"][:, HD:]
            qg = d["q"] * jnp.exp(d["gc_col"])
            if carry:
                d["s"] = s_scr[h]
                sb = d["s"].astype(BF16)
                d["ws"], d["qs"] = _dot(w, sb), _dot(qg, sb)
            else:
                ws, qs = [], []
                for s in range(spc):
                    sl = slice(s * ls, (s + 1) * ls)
                    rr = _dot(jnp.concatenate([w[sl], qg[sl]], axis=0), s0_ref[s, h])
                    ws.append(rr[:ls])
                    qs.append(rr[ls:])
                d["ws"], d["qs"] = jnp.concatenate(ws, axis=0), jnp.concatenate(qs, axis=0)
        for h in hs:
            d = st[h]
            cols = slice(h * HD, (h + 1) * HD)
            v_new = d["u"] - d["ws"]
            o = d["qs"] + _dot(d["qk"], v_new)
            kd = d["k"] * jnp.exp(d["gl_col"] - d["gc_col"])
            eg = jnp.exp(d["gl_col"])
            if carry:
                s_scr[h] = eg[0:1, :] * d["s"] + _dot_tn(kd, v_new)
            else:
                vb = v_new.astype(BF16)
                for s in range(spc):
                    kd_s = jnp.where(_seq_mask(ch, ls, s), kd, 0.0)
                    s_ref[s, h] = eg[s * ls:s * ls + 1, :] * s0_ref[s, h] + _dot_tn(kd_s, vb)
            y_out = o * lax.rsqrt(jnp.mean(o * o, axis=-1, keepdims=True) + EPS) * gn
            y_out = y_out * _silu(_rows(gate_ref, c, ch, cols))
            if c is None:
                o_ref[:, cols] = y_out
            else:
                o_ref[pl.ds(pl.multiple_of(c * ch, ch), ch), cols] = y_out

    _chunk_loop(nck, chunk)

    if carry:
        rows = nck * ch
        for a, x_ref in enumerate((q_ref, k_ref, v_ref)):
            tail_scr[:, a * GW:(a + 1) * GW] = x_ref[rows - 8:rows, :]

        @pl.when(r == pl.num_programs(1) - 1)
        def _():
            s_ref[0] = s_scr[...]


def _gdn(proj, buf8, s0, prm, layer, state_layer, prev, geom):
    g = geom
    smem = pl.BlockSpec(memory_space=pltpu.SMEM)
    in_specs = [smem, smem,
                g.row_spec(GW, CB_GQ), g.row_spec(GW, CB_GK), g.row_spec(GW, CB_GV), g.row_spec(GW, CB_GG),
                g.row_spec(HD, CB_SMALL),
                pl.BlockSpec((None, 8, 3 * GW), lambda i, r: (layer, 0, 0)),
                g.state_in(state_layer, (8, 3 * GW)),
                g.state_in(state_layer, (GH, HD, HD)),
                pl.BlockSpec((None, 1, HD), lambda i, r: (layer, 0, 0))]
    scratch = [pltpu.VMEM((GH, HD, HD), F32), pltpu.VMEM((8, 3 * GW), F32)] if g.carry else []
    return _mixer_call(
        functools.partial(_gdn_kernel, ch=g.ch, ls=g.ls, nck=g.nck, carry=g.carry), g,
        (prm["a_log"][layer], prm["dt_bias"][layer], proj, proj, proj, proj, proj, prm["conv_w8"], buf8, s0,
         prm["gdn_g"]),
        in_specs,
        [g.row_spec(GW, 0), g.state_out(layer, (GH, HD, HD))],
        [jax.ShapeDtypeStruct((g.n_seq * g.t, GW), F32), g.stack_shape((GH, HD, HD))],
        scratch, prev, "gdn")


def _mlstm_kernel(igb_ref, fgb_ref, q_ref, k_ref, v_ref, og_ref, sm_ref, c0_ref, n0_ref, m0_ref, mg_ref,
                  o_ref, c_ref, n_ref, m_ref, *scratch, ch, ls, nck, carry):
    r = pl.program_id(1)
    mk = _Masks(ch, ls)
    spc = ch // ls
    neg_inf = F32(-jnp.inf)
    if carry:
        c_scr, n_scr, m_scr = scratch

        @pl.when(r == 0)
        def _():
            c_scr[...] = c0_ref[0]
            n_scr[...] = n0_ref[0]
            m_scr[...] = m0_ref[0]
    else:
        rows1 = lax.broadcasted_iota(jnp.int32, (ch, 1), 0)

        def per_seq_col(vals):
            out = jnp.zeros((ch, 1), F32)
            for s in range(spc):
                out = jnp.where((rows1 >= s * ls) & (rows1 < (s + 1) * ls), vals[s], out)
            return out

    def chunk(c):
        sm = _rows(sm_ref, c, ch)
        hs = range(MH)
        st = []
        for h in hs:
            cols = slice(h * HD, (h + 1) * HD)
            q = _rows(q_ref, c, ch, cols)
            k = _rows(k_ref, c, ch, cols) * (HD ** -0.5)
            v = _rows(v_ref, c, ch, cols)
            ig = sm[:, SM_I + h:SM_I + h + 1] + igb_ref[h]
            lf = -_softplus(-(sm[:, SM_F + h:SM_F + h + 1] + fgb_ref[h]))
            b_col, b_row, b_tot = mk.cumsum(lf)
            logd = jnp.where(mk.incl, b_col - b_row + mk.to_row(ig), neg_inf)
            rowmax = jnp.max(logd, axis=1, keepdims=True)
            d = dict(k=k, v=v, ig=ig, b_col=b_col, b_tot=b_tot)
            if carry:
                d["cm"], d["n"] = c_scr[h], n_scr[h]
                m_in = m_scr[h][:, 0:1]
                cq = _dot(q, d["cm"])
                nq = jnp.sum(q * d["n"], axis=1, keepdims=True)
            else:
                m_in = per_seq_col([m0_ref[s, h][:, 0:1] for s in range(spc)])
                cq = jnp.concatenate(
                    [_dot(q[s * ls:(s + 1) * ls], c0_ref[s, h]) for s in range(spc)], axis=0)
                nq = jnp.concatenate(
                    [jnp.sum(q[s * ls:(s + 1) * ls] * n0_ref[s, h], axis=1, keepdims=True) for s in range(spc)],
                    axis=0)
            d["inter"] = b_col + m_in
            d["m_t"] = jnp.maximum(d["inter"], rowmax)
            d["sc"] = _dot_nt(q, k) * jnp.exp(logd - d["m_t"])
            w_inter = jnp.exp(d["inter"] - d["m_t"])
            d["cq"], d["nq"] = w_inter * cq, w_inter * nq
            st.append(d)
        for h in hs:
            d = st[h]
            cols = slice(h * HD, (h + 1) * HD)
            num = _dot(d["sc"], d["v"]) + d["cq"]
            den = jnp.sum(d["sc"], axis=1, keepdims=True) + d["nq"]
            hh = num / jnp.maximum(jnp.abs(den), jnp.exp(-d["m_t"]))
            y = hh * lax.rsqrt(jnp.mean(hh * hh, axis=-1, keepdims=True) + EPS) * mg_ref[h]
            y = y * _sigmoid(_rows(og_ref, c, ch, cols))
            if c is None:
                o_ref[:, cols] = y
            else:
                o_ref[pl.ds(pl.multiple_of(c * ch, ch), ch), cols] = y
        for h in hs:
            d = st[h]
            k, v, ig, b_col, b_tot, inter, m_t = (d[n] for n in ("k", "v", "ig", "b_col", "b_tot", "inter", "m_t"))
            if carry:
                m_new = m_t[ch - 1:ch, :]
                dec = jnp.exp(inter[ch - 1:ch, :] - m_new)
                kw = k * jnp.exp(b_tot - b_col + ig - m_new)
                c_scr[h] = dec * d["cm"] + _dot_tn(kw, v)
                n_scr[h] = dec * d["n"] + jnp.sum(kw, axis=0, keepdims=True)
                m_scr[h] = jnp.broadcast_to(m_new, (1, HD))
            else:
                last = lambda a, s: a[(s + 1) * ls - 1:(s + 1) * ls, :]
                m_new = per_seq_col([last(m_t, s) for s in range(spc)])
                dec_col = jnp.exp(per_seq_col([last(inter, s) for s in range(spc)]) - m_new)
                kw = k * jnp.exp(b_tot - b_col + ig - m_new)
                vb = v.astype(BF16)
                for s in range(spc):
                    kw_s = jnp.where(_seq_mask(ch, ls, s), kw, 0.0)
                    dec = dec_col[s * ls:s * ls + 1, :]
                    c_ref[s, h] = dec * c0_ref[s, h] + _dot_tn(kw_s, vb)
                    n_ref[s, h] = dec * n0_ref[s, h] + jnp.sum(kw_s, axis=0, keepdims=True)
                    m_ref[s, h] = jnp.broadcast_to(last(m_t, s), (1, HD))

    _chunk_loop(nck, chunk)

    if carry:
        @pl.when(r == pl.num_programs(1) - 1)
        def _():
            c_ref[0] = c_scr[...]
            n_ref[0] = n_scr[...]
            m_ref[0] = m_scr[...]


def _mlstm(proj, c0, n0, m0, prm, layer, state_layer, prev, geom):
    g = geom
    smem = pl.BlockSpec(memory_space=pltpu.SMEM)
    in_specs = [smem, smem,
                g.row_spec(MW, CB_MQ), g.row_spec(MW, CB_MK), g.row_spec(MW, CB_MV), g.row_spec(MW, CB_MO),
                g.row_spec(HD, CB_SMALL),
                g.state_in(state_layer, (MH, HD, HD)), g.state_in(state_layer, (MH, 1, HD)),
                g.state_in(state_layer, (MH, 1, HD)),
                pl.BlockSpec((None, MH, 1, HD), lambda i, r: (layer, 0, 0, 0))]
    scratch = ([pltpu.VMEM((MH, HD, HD), F32), pltpu.VMEM((MH, 1, HD), F32), pltpu.VMEM((MH, 1, HD), F32)]
               if g.carry else [])
    return _mixer_call(
        functools.partial(_mlstm_kernel, ch=g.ch, ls=g.ls, nck=g.nck, carry=g.carry), g,
        (prm["ig_b"][layer], prm["fg_b"][layer], proj, proj, proj, proj, proj, c0, n0, m0, prm["mlstm_g"]),
        in_specs,
        [g.row_spec(MW, 0), g.state_out(layer, (MH, HD, HD)), g.state_out(layer, (MH, 1, HD)),
         g.state_out(layer, (MH, 1, HD))],
        [jax.ShapeDtypeStruct((g.n_seq * g.t, MW), F32), g.stack_shape((MH, HD, HD)),
         g.stack_shape((MH, 1, HD)), g.stack_shape((MH, 1, HD))],
        scratch, prev, "mlstm")


def _ret_kernel(lg_ref, q_ref, k_ref, v_ref, g_ref, cos_ref, sin_ref, s0_ref, rg_ref,
                o_ref, s_ref, *scratch, ch, ls, nck, carry):
    r = pl.program_id(1)
    mk = _Masks(ch, ls)
    spc = ch // ls
    pos = (lax.broadcasted_iota(jnp.int32, (ch, 1), 0) & (ls - 1)).astype(F32)
    if carry:
        (s_scr,) = scratch

        @pl.when(r == 0)
        def _():
            s_scr[...] = s0_ref[0]

    def rope(x, cos, sin):
        return x * cos + pltpu.roll(x, HD // 2, 1) * sin

    def chunk(c):
        cos = _rows(cos_ref, c, ch)
        sin = _rows(sin_ref, c, ch)
        hs = range(RH)
        st = []
        for h in hs:
            cols = slice(h * HD, (h + 1) * HD)
            lg = lg_ref[h]
            q = rope(_rows(q_ref, c, ch, cols), cos, sin)
            k = rope(_rows(k_ref, c, ch, cols), cos, sin) * (HD ** -0.5)
            v = _rows(v_ref, c, ch, cols)
            dmat = jnp.where(mk.incl, jnp.exp(jnp.where(mk.incl, mk.rel, 0.0) * lg), 0.0)
            qd = q * jnp.exp((pos + 1.0) * lg)
            d = dict(k=k, v=v, lg=lg, sc=_dot_nt(q, k) * dmat)
            if carry:
                d["s"] = s_scr[h]
                d["qs"] = _dot(qd, d["s"])
            else:
                d["qs"] = jnp.concatenate(
                    [_dot(qd[s * ls:(s + 1) * ls], s0_ref[s, h]) for s in range(spc)], axis=0)
            st.append(d)
        for h in hs:
            d = st[h]
            cols = slice(h * HD, (h + 1) * HD)
            k, v, lg = d["k"], d["v"], d["lg"]
            o = _dot(d["sc"], v) + d["qs"]
            kd = k * jnp.exp((ls - 1.0 - pos) * lg)
            cd = jnp.exp(jnp.full((1, 1), ls * lg, F32))
            if carry:
                s_scr[h] = cd * d["s"] + _dot_tn(kd, v)
            else:
                vb = v.astype(BF16)
                for s in range(spc):
                    kd_s = jnp.where(_seq_mask(ch, ls, s), kd, 0.0)
                    s_ref[s, h] = cd * s0_ref[s, h] + _dot_tn(kd_s, vb)
            mu = jnp.mean(o, axis=-1, keepdims=True)
            var = jnp.mean(jnp.square(o - mu), axis=-1, keepdims=True)
            y = (o - mu) * lax.rsqrt(var + EPS) * rg_ref[h] * _silu(_rows(g_ref, c, ch, cols))
            if c is None:
                o_ref[:, cols] = y
            else:
                o_ref[pl.ds(pl.multiple_of(c * ch, ch), ch), cols] = y

    _chunk_loop(nck, chunk)

    if carry:
        @pl.when(r == pl.num_programs(1) - 1)
        def _():
            s_ref[0] = s_scr[...]


def _ret(proj, s0, cos, sin, prm, layer, state_layer, prev, geom):
    g = geom
    if g.carry:
        tab = pl.BlockSpec((g.rows, HD), lambda i, r: (r, 0))
    else:
        tab = pl.BlockSpec((g.rows, HD), lambda i, r: (0, 0))
    in_specs = [pl.BlockSpec(memory_space=pltpu.SMEM),
                g.row_spec(RW, CB_RQ), g.row_spec(RW, CB_RK), g.row_spec(RW, CB_RV), g.row_spec(RW, CB_RG),
                tab, tab, g.state_in(state_layer, (RH, HD, HD)),
                pl.BlockSpec((None, RH, 1, HD), lambda i, r: (layer, 0, 0, 0))]
    scratch = [pltpu.VMEM((RH, HD, HD), F32)] if g.carry else []
    return _mixer_call(
        functools.partial(_ret_kernel, ch=g.ch, ls=g.ls, nck=g.nck, carry=g.carry), g,
        (prm["log_gamma"], proj, proj, proj, proj, cos, sin, s0, prm["ret_g"]),
        in_specs,
        [g.row_spec(RW, 0), g.state_out(layer, (RH, HD, HD))],
        [jax.ShapeDtypeStruct((g.n_seq * g.t, RW), F32), g.stack_shape((RH, HD, HD))],
        scratch, prev, "retention")


def _outproj_kernel(x_ref, go_ref, mh_ref, ro_ref, wg_ref, wm_ref, wr_ref, o_ref):
    acc = jnp.dot(go_ref[...].astype(BF16), wg_ref[...], preferred_element_type=F32)
    acc = acc + jnp.dot(mh_ref[...].astype(BF16), wm_ref[...], preferred_element_type=F32)
    acc = acc + jnp.dot(ro_ref[...].astype(BF16), wr_ref[...], preferred_element_type=F32)
    o_ref[...] = x_ref[...] + acc


def _outproj(x, go, mh, ro, wg, wm, wr, layer, *, tm, tn):
    rows = x.shape[0]
    return pl.pallas_call(
        _outproj_kernel,
        grid=(rows // tm, D_MODEL // tn),
        in_specs=[pl.BlockSpec((tm, tn), lambda i, j: (i, j)),
                  pl.BlockSpec((tm, GW), lambda i, j: (i, 0)),
                  pl.BlockSpec((tm, MW), lambda i, j: (i, 0)),
                  pl.BlockSpec((tm, RW), lambda i, j: (i, 0)),
                  pl.BlockSpec((None, GW, tn), lambda i, j: (layer, 0, j)),
                  pl.BlockSpec((None, MW, tn), lambda i, j: (layer, 0, j)),
                  pl.BlockSpec((None, RW, tn), lambda i, j: (layer, 0, j))],
        out_specs=pl.BlockSpec((tm, tn), lambda i, j: (i, j)),
        out_shape=jax.ShapeDtypeStruct((rows, D_MODEL), F32),
        compiler_params=_cparams(("parallel", "arbitrary")),
        name="outproj",
    )(x, go, mh, ro, wg, wm, wr)


def _ffn_kernel(x_ref, g_ref, wg_ref, wu_ref, wd_ref, o_ref, h_ref):
    j = pl.program_id(1)

    @pl.when(j == 0)
    def _():
        x = x_ref[...]
        y = x * lax.rsqrt(jnp.mean(x * x, axis=-1, keepdims=True) + EPS)
        h_ref[...] = (y * g_ref[...]).astype(BF16)
        o_ref[...] = x

    hb = h_ref[...]
    a = jnp.dot(hb, wg_ref[...], preferred_element_type=F32)
    b = jnp.dot(hb, wu_ref[...], preferred_element_type=F32)
    ff = (_silu(a) * b).astype(BF16)
    o_ref[...] += jnp.dot(ff, wd_ref[...], preferred_element_type=F32)


def _ffn(x, g_all, wg, wu, wd, layer, *, tm, th):
    rows = x.shape[0]
    return pl.pallas_call(
        _ffn_kernel,
        grid=(rows // tm, FFN_HIDDEN // th),
        in_specs=[pl.BlockSpec((tm, D_MODEL), lambda i, j: (i, 0)),
                  pl.BlockSpec((None, 1, D_MODEL), lambda i, j: (layer, 0, 0)),
                  pl.BlockSpec((None, D_MODEL, th), lambda i, j: (layer, 0, j)),
                  pl.BlockSpec((None, D_MODEL, th), lambda i, j: (layer, 0, j)),
                  pl.BlockSpec((None, th, D_MODEL), lambda i, j: (layer, j, 0))],
        out_specs=pl.BlockSpec((tm, D_MODEL), lambda i, j: (i, 0)),
        out_shape=jax.ShapeDtypeStruct((rows, D_MODEL), F32),
        scratch_shapes=[pltpu.VMEM((tm, D_MODEL), BF16)],
        compiler_params=_cparams(("parallel", "arbitrary")),
        name="ffn",
    )(x, g_all, wg, wu, wd)


def _norm_kernel(x_ref, g_ref, o_ref):
    x = x_ref[...]
    o_ref[...] = x * lax.rsqrt(jnp.mean(x * x, axis=-1, keepdims=True) + EPS) * g_ref[...]


def _final_norm(x, g, *, tm):
    rows = x.shape[0]
    return pl.pallas_call(
        _norm_kernel,
        grid=(rows // tm,),
        in_specs=[pl.BlockSpec((tm, D_MODEL), lambda i: (i, 0)),
                  pl.BlockSpec((1, D_MODEL), lambda i: (0, 0))],
        out_specs=pl.BlockSpec((tm, D_MODEL), lambda i: (i, 0)),
        out_shape=jax.ShapeDtypeStruct((rows, D_MODEL), F32),
        compiler_params=_cparams(("parallel",)),
        name="final_norm",
    )(x, g)


def _relayout_w_in(w_in):
    sizes = (3 * GW, GW, GH, GH, 3 * MW, MW, MH, MH, 3 * RW)
    gq, gg, gb, ga, mq, mo, mi, mf, rq, rg = jnp.split(
        w_in, [sum(sizes[:n + 1]) for n in range(len(sizes))], axis=-1)
    pad = jnp.zeros(w_in.shape[:-1] + (2 * HD - (2 * GH + 2 * MH),), w_in.dtype)
    return jnp.concatenate([mq, mo, rq, rg, gb, ga, mi, mf, pad, gq, gg], axis=-1).astype(BF16)


def _rope_tables(pos0, t):
    half = HD // 2
    inv = ROPE_BASE ** (-jnp.arange(half, dtype=F32) / half)
    ang = (pos0 + jnp.arange(t)).astype(F32)[:, None] * inv[None, :]
    cos, sin = jnp.cos(ang), jnp.sin(ang)
    return jnp.concatenate([cos, cos], axis=-1), jnp.concatenate([-sin, sin], axis=-1)


def _trunk(x, states, state_layered, pos0, prm, *, n_seq, t, ch, ls, tm):
    conv8, g_s, m_c, m_n, m_m, r_s = states
    geom = _Geom(n_seq, t, ch, ls)
    cos, sin = _rope_tables(pos0, t)
    if not geom.carry:
        cos, sin = jnp.tile(cos, (geom.nb, 1)), jnp.tile(sin, (geom.nb, 1))
    g_prev = m_prev = r_prev = None
    convs = []
    for l in range(DEPTH):
        sl = l if state_layered else None
        proj = _inproj(x, prm["attn_g"], prm["w_in"], l, tm=tm, tn=768)
        go, *g_prev = _gdn(proj, conv8, g_s, prm, l, sl, g_prev, geom)
        mh, *m_prev = _mlstm(proj, m_c, m_n, m_m, prm, l, sl, m_prev, geom)
        ro, *r_prev = _ret(proj, r_s, cos, sin, prm, l, sl, r_prev, geom)
        x = _outproj(x, go, mh, ro, prm["wo_g"], prm["wo_m"], prm["wo_r"], l, tm=tm, tn=1024)
        x = _ffn(x, prm["ffn_g"], prm["w_gate"], prm["w_up"], prm["w_down"], l, tm=tm, th=512)
        convs.append(proj.reshape(n_seq, t, PROJ_W)[:, t - (CONV_W - 1):, GDN_COL0:GDN_COL0 + 3 * GW])
    y = _final_norm(x, prm["final_g"], tm=tm)
    m_c_out, m_n_out, m_m_out = m_prev
    return y, [jnp.stack(convs), g_prev[0], m_c_out, m_n_out[:, :, :, 0, :], m_m_out[:, :, :, 0, 0], r_prev[0]]


def kernel(x_prompt, x_sample, state_gdn_conv, state_gdn_S, state_mlstm_C, state_mlstm_n, state_mlstm_m,
           state_ret_S, attn_norm_g, w_in, gdn_conv_w, gdn_a_log, gdn_dt_bias, gdn_norm_g, mlstm_ig_bias,
           mlstm_fg_bias, mlstm_norm_g, ret_norm_g, w_out, ffn_norm_g, w_gate, w_up, w_down, final_norm_g):
    bp, tp, _ = x_prompt.shape
    bs, ts, _ = x_sample.shape
    prm = {
        "attn_g": attn_norm_g[:, None, :],
        "w_in": _relayout_w_in(w_in),
        "conv_w8": jnp.pad(gdn_conv_w, ((0, 0), (0, 8 - CONV_W), (0, 0))),
        "a_log": gdn_a_log, "dt_bias": gdn_dt_bias,
        "gdn_g": gdn_norm_g[:, None, :],
        "ig_b": mlstm_ig_bias, "fg_b": mlstm_fg_bias,
        "mlstm_g": mlstm_norm_g[:, :, None, :],
        "ret_g": ret_norm_g[:, :, None, :],
        "log_gamma": jnp.log1p(-jnp.exp2(-(RET_DECAY_OFFSET + jnp.arange(RH, dtype=F32)))),
        "wo_g": w_out[:, :GW].astype(BF16),
        "wo_m": w_out[:, GW:GW + MW].astype(BF16),
        "wo_r": w_out[:, GW + MW:].astype(BF16),
        "ffn_g": ffn_norm_g[:, None, :],
        "w_gate": w_gate.astype(BF16), "w_up": w_up.astype(BF16), "w_down": w_down.astype(BF16),
        "final_g": final_norm_g[None, :],
    }
    zeros = lambda *s: jnp.zeros(s, F32)
    p_states = (zeros(bp, 8, 3 * GW), zeros(bp, GH, HD, HD), zeros(bp, MH, HD, HD),
                zeros(bp, MH, 1, HD), zeros(bp, MH, 1, HD), zeros(bp, RH, HD, HD))
    s_states = (jnp.pad(state_gdn_conv, ((0, 0), (0, 0), (8 - (CONV_W - 1), 0), (0, 0))),
                state_gdn_S, state_mlstm_C, state_mlstm_n[:, :, :, None, :],
                jnp.broadcast_to(state_mlstm_m[:, :, :, None, None], state_mlstm_m.shape + (1, HD)),
                state_ret_S)
    lp = math.gcd(tp, CHUNK)
    ls = math.gcd(ts, CHUNK)
    yp, pst = _trunk(x_prompt.reshape(bp * tp, D_MODEL), p_states, False, 0, prm,
                     n_seq=bp, t=tp, ch=lp, ls=lp, tm=1024)
    ys, sst = _trunk(x_sample.reshape(bs * ts, D_MODEL), s_states, True, PAST_LEN, prm,
                     n_seq=bs, t=ts, ch=HD, ls=ls, tm=1024)
    return (yp.reshape(bp, tp, D_MODEL), ys.reshape(bs, ts, D_MODEL), *pst, *sst)
```
